```python
import jax, jax.numpy as jnp
from jax import lax
import numpy as np

D_MODEL = 1024
BATCH = 4
SEQ = 4096
DEPTH = 4
DEC_BATCH = 16
DEC_SEQ = 16
PAST_LEN = 4096

CHUNK = 64
LEFT_CHUNKS = 8
BAND_LEFT = LEFT_CHUNKS * CHUNK
BAND = BAND_LEFT + CHUNK
N_HEADS = 16
HEAD_DIM = D_MODEL // N_HEADS
CONV_WIDTH = 31
CONV_HIST = CONV_WIDTH - 1
FFN_HIDDEN = -(-8 * D_MODEL // (3 * 256)) * 256
REL_CLIP = 128
N_REL = 2 * REL_CLIP + 1
N_A_LAYERS = DEPTH // 2
N_B_LAYERS = DEPTH - N_A_LAYERS
EPS = 1e-6
NEG_INF = -1e30

kernel_name = "yoco_conformer_chunk_band_attention_stream_step"


def rms_norm(x, g):
    xf = x.astype(jnp.float32)
    y = xf * lax.rsqrt(jnp.mean(xf * xf, axis=-1, keepdims=True) + EPS)
    return (y * g.astype(jnp.float32)).astype(x.dtype)


def layer_norm(x, g, b):
    xf = x.astype(jnp.float32)
    mu = jnp.mean(xf, axis=-1, keepdims=True)
    var = jnp.mean(jnp.square(xf - mu), axis=-1, keepdims=True)
    y = (xf - mu) * lax.rsqrt(var + EPS)
    return (y * g.astype(jnp.float32) + b.astype(jnp.float32)).astype(x.dtype)


def swiglu_ffn(h, g, w_in, w_out):
    a, b = jnp.split(rms_norm(h, g) @ w_in, 2, axis=-1)
    return h + (jax.nn.silu(a) * b) @ w_out


def conv_module(h, hist, g, w_pw1, b_pw1, w_dw, b_dw, ln_g, ln_b, w_pw2, b_pw2):
    a, gate = jnp.split(rms_norm(h, g) @ w_pw1 + b_pw1, 2, axis=-1)
    glu = a * jax.nn.sigmoid(gate)
    full = jnp.concatenate([hist.astype(glu.dtype), glu], axis=1)
    y = lax.conv_general_dilated(
        full, w_dw[:, None, :].astype(full.dtype), window_strides=(1,), padding="VALID",
        dimension_numbers=("NWC", "WIO", "NWC"), feature_group_count=D_MODEL) + b_dw
    y = jax.nn.silu(layer_norm(y, ln_g, ln_b))
    return h + y @ w_pw2 + b_pw2, full[:, -CONV_HIST:]


def rel_bias_band(table, n_q, n_k, left):
    dist = jnp.arange(n_q)[:, None] + left - jnp.arange(n_k)[None, :]
    idx = jnp.clip(dist, -REL_CLIP, REL_CLIP) + REL_CLIP
    return table[:, idx].astype(jnp.float32)


def band_attention_prompt(q, k, v, table):
    B, T, H, Dh = q.shape
    nc = T // CHUNK
    scale = HEAD_DIM ** -0.5
    pad = ((0, 0), (BAND_LEFT, 0), (0, 0), (0, 0))
    kp = jnp.pad(k, pad)
    vp = jnp.pad(v, pad)
    bias = rel_bias_band(table, CHUNK, BAND, BAND_LEFT)
    q_chunks = q.reshape(B, nc, CHUNK, H, Dh).transpose(1, 0, 2, 3, 4)

    def one_chunk(args):
        c, q_blk = args
        start = c * CHUNK
        k_blk = lax.dynamic_slice_in_dim(kp, start, BAND, axis=1)
        v_blk = lax.dynamic_slice_in_dim(vp, start, BAND, axis=1)
        s = jnp.einsum("bqhd,bkhd->bhqk", q_blk, k_blk).astype(jnp.float32) * scale + bias
        valid = (start - BAND_LEFT + jnp.arange(BAND)) >= 0
        s = jnp.where(valid[None, None, None, :], s, NEG_INF)
        p = jax.nn.softmax(s, axis=-1).astype(v_blk.dtype)
        return jnp.einsum("bhqk,bkhd->bqhd", p, v_blk)

    out = lax.map(one_chunk, (jnp.arange(nc), q_chunks))
    return out.transpose(1, 0, 2, 3, 4).reshape(B, T, H * Dh)


def band_attention_sample(q, k_all, v_all, table):
    B, S, H, Dh = q.shape
    W = k_all.shape[1] - S
    scale = HEAD_DIM ** -0.5
    bias = rel_bias_band(table, S, W + S, W)
    s = jnp.einsum("bqhd,bkhd->bhqk", q, k_all).astype(jnp.float32) * scale + bias
    p = jax.nn.softmax(s, axis=-1).astype(v_all.dtype)
    return jnp.einsum("bhqk,bkhd->bqhd", p, v_all).reshape(B, S, H * Dh)


def run_trunk(x, conv_hist, past_k, past_v,
              norm_conv, w_pw1, b_pw1, w_dw, b_dw, ln_g, ln_b, w_pw2, b_pw2,
              norm_kv, w_kv, norm_attn, w_q, w_o, rel_bias,
              norm_ffn, w_ffn_in, w_ffn_out, norm_final):
    B, T, _ = x.shape
    h = x
    new_hist = []
    k = v = None
    for l in range(DEPTH):
        if l < N_A_LAYERS:
            h, st = conv_module(h, conv_hist[l], norm_conv[l], w_pw1[l], b_pw1[l], w_dw[l],
                                b_dw[l], ln_g[l], ln_b[l], w_pw2[l], b_pw2[l])
            new_hist.append(st)
        else:
            if l == N_A_LAYERS:
                k, v = jnp.split(rms_norm(h, norm_kv) @ w_kv, 2, axis=-1)
                k = k.reshape(B, T, N_HEADS, HEAD_DIM)
                v = v.reshape(B, T, N_HEADS, HEAD_DIM)
            j = l - N_A_LAYERS
            q = (rms_norm(h, norm_attn[j]) @ w_q[j]).reshape(B, T, N_HEADS, HEAD_DIM)
            if past_k is None:
                o = band_attention_prompt(q, k, v, rel_bias[j])
            else:
                o = band_attention_sample(q, jnp.concatenate([past_k.astype(k.dtype), k], axis=1),
                                          jnp.concatenate([past_v.astype(v.dtype), v], axis=1),
                                          rel_bias[j])
            h = h + o @ w_o[j]
        h = swiglu_ffn(h, norm_ffn[l], w_ffn_in[l], w_ffn_out[l])
    return rms_norm(h, norm_final), jnp.stack(new_hist, axis=0), k, v


def setup_inputs(seed: int = 0) -> dict:
    key = jax.random.key(seed)
    ks = jax.random.split(key, 32)
    D, F, HD = D_MODEL, FFN_HIDDEN, N_HEADS * HEAD_DIM
    cache_rows = min(BAND_LEFT, PAST_LEN)

    def nrm(k, shape, scale):
        return jax.random.normal(k, shape, jnp.float32) * scale

    return {
        "x_prompt": nrm(ks[0], (BATCH, SEQ, D), 1.0),
        "x_sample": nrm(ks[1], (DEC_BATCH, DEC_SEQ, D), 1.0),
        "cache_conv": nrm(ks[2], (N_A_LAYERS, DEC_BATCH, CONV_HIST, D), 0.5),
        "cache_k": nrm(ks[3], (DEC_BATCH, cache_rows, N_HEADS, HEAD_DIM), 1.0),
        "cache_v": nrm(ks[4], (DEC_BATCH, cache_rows, N_HEADS, HEAD_DIM), 1.0),
        "norm_conv": 1.0 + nrm(ks[5], (N_A_LAYERS, D), 0.01),
        "w_pw1": nrm(ks[6], (N_A_LAYERS, D, 2 * D), D ** -0.5),
        "b_pw1": nrm(ks[7], (N_A_LAYERS, 2 * D), 0.01),
        "w_dw": nrm(ks[8], (N_A_LAYERS, CONV_WIDTH, D), CONV_WIDTH ** -0.5),
        "b_dw": nrm(ks[9], (N_A_LAYERS, D), 0.01),
        "ln_g": 1.0 + nrm(ks[10], (N_A_LAYERS, D), 0.01),
        "ln_b": nrm(ks[11], (N_A_LAYERS, D), 0.01),
        "w_pw2": nrm(ks[12], (N_A_LAYERS, D, D), D ** -0.5),
        "b_pw2": nrm(ks[13], (N_A_LAYERS, D), 0.01),
        "norm_kv": 1.0 + nrm(ks[14], (D,), 0.01),
        "w_kv": nrm(ks[15], (D, 2 * HD), D ** -0.5),
        "norm_attn": 1.0 + nrm(ks[16], (N_B_LAYERS, D), 0.01),
        "w_q": nrm(ks[17], (N_B_LAYERS, D, HD), D ** -0.5),
        "w_o": nrm(ks[18], (N_B_LAYERS, HD, D), HD ** -0.5),
        "rel_bias": nrm(ks[19], (N_B_LAYERS, N_HEADS, N_REL), 0.2),
        "norm_ffn": 1.0 + nrm(ks[20], (DEPTH, D), 0.01),
        "w_ffn_in": nrm(ks[21], (DEPTH, D, 2 * F), D ** -0.5),
        "w_ffn_out": nrm(ks[22], (DEPTH, F, D), F ** -0.5),
        "norm_final": 1.0 + nrm(ks[23], (D,), 0.01),
    }


def reference(x_prompt, x_sample, cache_conv, cache_k, cache_v,
              norm_conv, w_pw1, b_pw1, w_dw, b_dw, ln_g, ln_b, w_pw2, b_pw2,
              norm_kv, w_kv, norm_attn, w_q, w_o, rel_bias,
              norm_ffn, w_ffn_in, w_ffn_out, norm_final):
    weights = (norm_conv, w_pw1, b_pw1, w_dw, b_dw, ln_g, ln_b, w_pw2, b_pw2,
               norm_kv, w_kv, norm_attn, w_q, w_o, rel_bias,
               norm_ffn, w_ffn_in, w_ffn_out, norm_final)
    B, T, _ = x_prompt.shape
    zero_hist = jnp.zeros((N_A_LAYERS, B, CONV_HIST, D_MODEL), x_prompt.dtype)
    y_prompt, conv_prompt, k_full, v_full = run_trunk(x_prompt, zero_hist, None, None, *weights)
    keep = min(BAND_LEFT, T)
    k_prompt = k_full[:, T - keep:]
    v_prompt = v_full[:, T - keep:]
    y_sample, conv_sample, k_sample, v_sample = run_trunk(x_sample, cache_conv, cache_k, cache_v, *weights)
    return (y_prompt, y_sample, conv_prompt, conv_sample, k_prompt, v_prompt, k_sample, v_sample)
```

```python
import functools

import jax
import jax.numpy as jnp
from jax import lax
from jax.experimental import pallas as pl
from jax.experimental.pallas import tpu as pltpu

D_MODEL = 1024
N_HEADS = 16
HEAD_DIM = 64
FFN_HIDDEN = 2816
CHUNK = 64
BAND_LEFT = 512
CONV_WIDTH = 31
CONV_HIST = 30
HIST_PAD = 32
REL_CLIP = 128
N_REL = 2 * REL_CLIP + 1
EPS = 1e-6
NEG_INF = -1e30

Q_BLOCK = 256
K_WINDOW = Q_BLOCK + BAND_LEFT
HEADS_PER_GROUP = 4
GROUP_LANES = HEADS_PER_GROUP * HEAD_DIM
ROW_EXT = 1024

VMEM_LIMIT = 56 * 1024 * 1024

F32 = jnp.float32
BF16 = jnp.bfloat16


def _resident(shape):
    nd = len(shape)
    return pl.BlockSpec(shape, lambda *_: (0,) * nd, pipeline_mode=pl.Buffered(1))


def _params(sem):
    return pltpu.CompilerParams(dimension_semantics=sem, vmem_limit_bytes=VMEM_LIMIT)


def _rms(x, g):
    ms = jnp.mean(x * x, axis=-1, keepdims=True)
    return x * lax.rsqrt(ms + EPS) * g


def _sigmoid(x):
    return 1.0 / (1.0 + jnp.exp(-x))


def _dot(a, b):
    return jnp.dot(a, b, preferred_element_type=F32)


def _dot_t(a, b):
    return lax.dot_general(a, b, (((1,), (1,)), ((), ())), preferred_element_type=F32)


def _layer_norm_swish(acc, ln_g, ln_b):
    mu = jnp.mean(acc, axis=-1, keepdims=True)
    xc = acc - mu
    var = jnp.mean(xc * xc, axis=-1, keepdims=True)
    y = xc * lax.rsqrt(var + EPS) * ln_g + ln_b
    return y * _sigmoid(y)


CONV_TM = 512
CONV_ROWS_MM = 256
CONV_ROWS_DW = 32


def _conv_prompt_kernel(h_ref, hist_ref, g_ref, w1_ref, b1_ref, wdw_ref, bdw_ref,
                        lng_ref, lnb_ref, w2_ref, b2_ref,
                        out_ref, state_ref, fbuf, ybuf):
    t = pl.program_id(1)
    tm = CONV_TM

    @pl.when(t == 0)
    def _():
        fbuf[0:HIST_PAD, :] = hist_ref[0]

    g = g_ref[...]
    b1 = b1_ref[...]
    for c in range(tm // CONV_ROWS_MM):
        rows = slice(c * CONV_ROWS_MM, (c + 1) * CONV_ROWS_MM)
        xn = _rms(h_ref[0, rows, :], g).astype(BF16)
        ag = _dot(xn, w1_ref[...]) + b1
        glu = ag[:, :D_MODEL] * _sigmoid(ag[:, D_MODEL:])
        fbuf[HIST_PAD + c * CONV_ROWS_MM:HIST_PAD + (c + 1) * CONV_ROWS_MM, :] = glu

    bdw = bdw_ref[...]
    ln_g = lng_ref[...]
    ln_b = lnb_ref[...]

    def dw_body(i, carry):
        r0 = pl.multiple_of(i * CONV_ROWS_DW, CONV_ROWS_DW)
        slab = fbuf[pl.ds(r0, CONV_ROWS_DW + HIST_PAD), :]
        acc = jnp.zeros((CONV_ROWS_DW, D_MODEL), F32) + bdw
        for k in range(CONV_WIDTH):
            lo = HIST_PAD - CONV_HIST + k
            acc = acc + slab[lo:lo + CONV_ROWS_DW, :] * wdw_ref[k:k + 1, :]
        ybuf[pl.ds(r0, CONV_ROWS_DW), :] = _layer_norm_swish(acc, ln_g, ln_b).astype(BF16)
        return carry

    lax.fori_loop(0, tm // CONV_ROWS_DW, dw_body, 0)

    b2 = b2_ref[...]
    for c in range(tm // CONV_ROWS_MM):
        rows = slice(c * CONV_ROWS_MM, (c + 1) * CONV_ROWS_MM)
        out_ref[0, rows, :] = h_ref[0, rows, :] + _dot(ybuf[rows, :], w2_ref[...]) + b2

    @pl.when(t == pl.num_programs(1) - 1)
    def _():
        state_ref[0] = fbuf[tm + HIST_PAD - CONV_HIST:tm + HIST_PAD, :]

    fbuf[0:HIST_PAD, :] = fbuf[tm:tm + HIST_PAD, :]


def _conv_prompt(h, hist, g, w1, b1, wdw, bdw, ln_g, ln_b, w2, b2):
    B, T, D = h.shape
    tm = CONV_TM
    vec = lambda n: _resident((1, n))
    return pl.pallas_call(
        _conv_prompt_kernel,
        grid=(B, T // tm),
        in_specs=[
            pl.BlockSpec((1, tm, D), lambda b, t: (b, t, 0)),
            pl.BlockSpec((1, HIST_PAD, D), lambda b, t: (b, 0, 0)),
            vec(D), _resident((D, 2 * D)), vec(2 * D), _resident((CONV_WIDTH, D)), vec(D),
            vec(D), vec(D), _resident((D, D)), vec(D),
        ],
        out_specs=[
            pl.BlockSpec((1, tm, D), lambda b, t: (b, t, 0)),
            pl.BlockSpec((1, CONV_HIST, D), lambda b, t: (b, 0, 0)),
        ],
        out_shape=[
            jax.ShapeDtypeStruct((B, T, D), F32),
            jax.ShapeDtypeStruct((B, CONV_HIST, D), F32),
        ],
        scratch_shapes=[
            pltpu.VMEM((tm + HIST_PAD, D), F32),
            pltpu.VMEM((tm, D), BF16),
        ],
        compiler_params=_params(("arbitrary", "arbitrary")),
        name="conv_prompt",
    )(h, hist, g, w1, b1, wdw, bdw, ln_g, ln_b, w2, b2)


def _conv_sample_kernel(h_ref, hist_ref, g_ref, w1_ref, b1_ref, wdw_ref, bdw_ref,
                        lng_ref, lnb_ref, w2_ref, b2_ref,
                        out_ref, state_ref, fbuf):
    nb, _, _ = hist_ref.shape
    ts = h_ref.shape[0] // nb
    x = h_ref[...]
    xn = _rms(x, g_ref[...]).astype(BF16)
    ag = _dot(xn, w1_ref[...]) + b1_ref[...]
    glu = ag[:, :D_MODEL] * _sigmoid(ag[:, D_MODEL:])
    fbuf[:, 0:HIST_PAD, :] = hist_ref[...]
    fbuf[:, HIST_PAD:HIST_PAD + ts, :] = glu.reshape(nb, ts, D_MODEL)
    acc = jnp.zeros((nb, ts, D_MODEL), F32) + bdw_ref[...]
    for k in range(CONV_WIDTH):
        lo = HIST_PAD - CONV_HIST + k
        acc = acc + fbuf[:, lo:lo + ts, :] * wdw_ref[k:k + 1, :]
    y = _layer_norm_swish(acc.reshape(nb * ts, D_MODEL), lng_ref[...], lnb_ref[...]).astype(BF16)
    out_ref[...] = x + _dot(y, w2_ref[...]) + b2_ref[...]
    state_ref[...] = fbuf[:, HIST_PAD + ts - CONV_HIST:HIST_PAD + ts, :]


def _conv_sample(h, hist, g, w1, b1, wdw, bdw, ln_g, ln_b, w2, b2):
    M, D = h.shape
    nb = hist.shape[0]
    ts = M // nb
    vec = lambda n: _resident((1, n))
    return pl.pallas_call(
        _conv_sample_kernel,
        grid=(1,),
        in_specs=[
            _resident((M, D)), _resident((nb, HIST_PAD, D)),
            vec(D), _resident((D, 2 * D)), vec(2 * D), _resident((CONV_WIDTH, D)), vec(D),
            vec(D), vec(D), _resident((D, D)), vec(D),
        ],
        out_specs=[
            pl.BlockSpec((M, D), lambda i: (0, 0)),
            pl.BlockSpec((nb, CONV_HIST, D), lambda i: (0, 0, 0)),
        ],
        out_shape=[
            jax.ShapeDtypeStruct((M, D), F32),
            jax.ShapeDtypeStruct((nb, CONV_HIST, D), F32),
        ],
        scratch_shapes=[pltpu.VMEM((nb, HIST_PAD + ts, D), F32)],
        compiler_params=_params(("arbitrary",)),
        name="conv_sample",
    )(h, hist, g, w1, b1, wdw, bdw, ln_g, ln_b, w2, b2)


FFN_TM = 512
FFN_FC = 256


def _ffn_kernel(*refs, with_o, with_final):
    it = iter(refs)
    h_ref = next(it)
    o_ref = next(it) if with_o else None
    wo_ref = next(it) if with_o else None
    g_ref = next(it)
    win_ref = next(it)
    wout_ref = next(it)
    gf_ref = next(it) if with_final else None
    out_ref = next(it)
    gbuf = next(it)

    x = h_ref[...]
    if with_o:
        x = x + _dot(o_ref[...], wo_ref[...])
    xn = _rms(x, g_ref[...]).astype(BF16)
    for c in range(FFN_HIDDEN // FFN_FC):
        a = _dot(xn, win_ref[:, c * FFN_FC:(c + 1) * FFN_FC])
        b = _dot(xn, win_ref[:, FFN_HIDDEN + c * FFN_FC:FFN_HIDDEN + (c + 1) * FFN_FC])
        gbuf[:, c * FFN_FC:(c + 1) * FFN_FC] = (a * _sigmoid(a) * b).astype(BF16)
    y = x + _dot(gbuf[...], wout_ref[...])
    if with_final:
        y = _rms(y, gf_ref[...])
    out_ref[...] = y


def _ffn(h, g, w_in, w_out, o=None, w_o=None, g_final=None):
    M, D = h.shape
    tm = min(FFN_TM, M)
    with_o = o is not None
    with_final = g_final is not None
    row = pl.BlockSpec((tm, D), lambda i: (i, 0))
    args = [h]
    specs = [row]
    if with_o:
        args += [o, w_o]
        specs += [row, _resident((D, D))]
    args += [g, w_in, w_out]
    specs += [_resident((1, D)), _resident((D, 2 * FFN_HIDDEN)), _resident((FFN_HIDDEN, D))]
    if with_final:
        args.append(g_final)
        specs.append(_resident((1, D)))
    return pl.pallas_call(
        functools.partial(_ffn_kernel, with_o=with_o, with_final=with_final),
        grid=(M // tm,),
        in_specs=specs,
        out_specs=row,
        out_shape=jax.ShapeDtypeStruct((M, D), F32),
        scratch_shapes=[pltpu.VMEM((tm, FFN_HIDDEN), BF16)],
        compiler_params=_params(("arbitrary",)),
        name="ffn",
    )(*args)


PROJ_TM = 512


def _proj_kvq_prompt_kernel(h_ref, gkv_ref, wkv_ref, gq_ref, wq_ref,
                            kpad_ref, vpad_ref, kf_ref, vf_ref, q_ref):
    t = pl.program_id(1)

    @pl.when(t == 0)
    def _():
        kpad_ref[...] = jnp.zeros_like(kpad_ref)
        vpad_ref[...] = jnp.zeros_like(vpad_ref)

    @pl.when(t > 0)
    def _():
        x = h_ref[0]
        ms = jnp.mean(x * x, axis=-1, keepdims=True)
        xr = x * lax.rsqrt(ms + EPS)
        kv = _dot((xr * gkv_ref[...]).astype(BF16), wkv_ref[...])
        k = kv[:, :D_MODEL]
        v = kv[:, D_MODEL:]
        kpad_ref[0] = k.astype(BF16)
        vpad_ref[0] = v.astype(BF16)
        q = _dot((xr * gq_ref[...]).astype(BF16), wq_ref[...])
        q_ref[0] = (q * (HEAD_DIM ** -0.5)).astype(BF16)

        @pl.when(t == pl.num_programs(1) - 1)
        def _():
            kf_ref[0] = k
            vf_ref[0] = v


def _proj_kvq_prompt(h, g_kv, w_kv, g_q, w_q):
    B, T, D = h.shape
    tm = PROJ_TM
    assert tm == BAND_LEFT
    nt = T // tm
    cur = lambda b, t: (b, jnp.maximum(t - 1, 0), 0)
    return pl.pallas_call(
        _proj_kvq_prompt_kernel,
        grid=(B, nt + 1),
        in_specs=[
            pl.BlockSpec((1, tm, D), cur),
            _resident((1, D)), _resident((D, 2 * D)), _resident((1, D)), _resident((D, D)),
        ],
        out_specs=[
            pl.BlockSpec((1, tm, D), lambda b, t: (b, t, 0)),
            pl.BlockSpec((1, tm, D), lambda b, t: (b, t, 0)),
            pl.BlockSpec((1, tm, D), lambda b, t: (b, 0, 0)),
            pl.BlockSpec((1, tm, D), lambda b, t: (b, 0, 0)),
            pl.BlockSpec((1, tm, D), cur),
        ],
        out_shape=[
            jax.ShapeDtypeStruct((B, T + BAND_LEFT, D), BF16),
            jax.ShapeDtypeStruct((B, T + BAND_LEFT, D), BF16),
            jax.ShapeDtypeStruct((B, tm, D), F32),
            jax.ShapeDtypeStruct((B, tm, D), F32),
            jax.ShapeDtypeStruct((B, T, D), BF16),
        ],
        compiler_params=_params(("arbitrary", "arbitrary")),
        name="proj_kvq_prompt",
    )(h, g_kv, w_kv, g_q, w_q)


def _proj_kernel(*refs, with_kv):
    it = iter(refs)
    h_ref = next(it)
    if with_kv:
        gkv_ref = next(it)
        wkv_ref = next(it)
    gq_ref = next(it)
    wq_ref = next(it)
    if with_kv:
        k_ref = next(it)
        v_ref = next(it)
    q_ref = next(it)
    x = h_ref[...]
    ms = jnp.mean(x * x, axis=-1, keepdims=True)
    xr = x * lax.rsqrt(ms + EPS)
    if with_kv:
        kv = _dot((xr * gkv_ref[...]).astype(BF16), wkv_ref[...])
        k_ref[...] = kv[:, :D_MODEL]
        v_ref[...] = kv[:, D_MODEL:]
    q = _dot((xr * gq_ref[...]).astype(BF16), wq_ref[...])
    q_ref[...] = (q * (HEAD_DIM ** -0.5)).astype(BF16)


def _proj(h, g_q, w_q, g_kv=None, w_kv=None):
    M, D = h.shape
    tm = min(PROJ_TM, M)
    with_kv = g_kv is not None
    row = pl.BlockSpec((tm, D), lambda i: (i, 0))
    args = [h]
    specs = [row]
    if with_kv:
        args += [g_kv, w_kv]
        specs += [_resident((1, D)), _resident((D, 2 * D))]
    args += [g_q, w_q]
    specs += [_resident((1, D)), _resident((D, D))]
    out_specs = [row]
    out_shape = [jax.ShapeDtypeStruct((M, D), BF16)]
    if with_kv:
        out_specs = [row, row] + out_specs
        out_shape = [jax.ShapeDtypeStruct((M, D), F32)] * 2 + out_shape
    return pl.pallas_call(
        functools.partial(_proj_kernel, with_kv=with_kv),
        grid=(M // tm,),
        in_specs=specs,
        out_specs=out_specs,
        out_shape=out_shape,
        compiler_params=_params(("arbitrary",)),
        name="proj",
    )(*args)


def _bias_kernel(tbl_ref, out_ref, rows_scr):
    m = lax.broadcasted_iota(jnp.int32, (N_HEADS, ROW_EXT), 1)
    mm = jnp.where(m < K_WINDOW, m, m - ROW_EXT)
    idx = jnp.clip(BAND_LEFT - mm, -REL_CLIP, REL_CLIP) + REL_CLIP
    rows = jnp.zeros((N_HEADS, ROW_EXT), F32)
    for t in range(N_REL):
        rows = jnp.where(idx == t, tbl_ref[0, :, t:t + 1], rows)
    rows_scr[...] = rows

    qi = lax.broadcasted_iota(jnp.int32, (Q_BLOCK, K_WINDOW), 0) // CHUNK
    kj = lax.broadcasted_iota(jnp.int32, (Q_BLOCK, K_WINDOW), 1) // CHUNK
    in_band = (kj >= qi) & (kj <= qi + BAND_LEFT // CHUNK)
    for h in range(N_HEADS):
        row = jnp.broadcast_to(rows_scr[h:h + 1, :], (Q_BLOCK, ROW_EXT))
        toe = pltpu.roll(row, 0, 1, stride=1, stride_axis=0)
        out_ref[0, h] = jnp.where(in_band, toe[:, :K_WINDOW], NEG_INF)


def _bias_bands(rel_bias):
    L = rel_bias.shape[0]
    return pl.pallas_call(
        _bias_kernel,
        grid=(L,),
        in_specs=[pl.BlockSpec((1, N_HEADS, N_REL), lambda l: (l, 0, 0))],
        out_specs=pl.BlockSpec((1, N_HEADS, Q_BLOCK, K_WINDOW), lambda l: (l, 0, 0, 0)),
        out_shape=jax.ShapeDtypeStruct((L, N_HEADS, Q_BLOCK, K_WINDOW), F32),
        scratch_shapes=[pltpu.VMEM((N_HEADS, ROW_EXT), F32)],
        compiler_params=_params(("arbitrary",)),
        name="bias_bands",
    )(rel_bias)


def _head_lane_masks(width):
    lane = lax.broadcasted_iota(jnp.int32, (1, width), 1)
    return [(lane >= hh * HEAD_DIM) & (lane < (hh + 1) * HEAD_DIM) for hh in range(HEADS_PER_GROUP)]


def _attn_prompt_kernel(q_ref, k0_ref, k1_ref, k2_ref, v0_ref, v1_ref, v2_ref, bias_ref, o_ref):
    j = pl.program_id(1)
    k_refs = (k0_ref, k1_ref, k2_ref)
    v_refs = (v0_ref, v1_ref, v2_ref)
    nk = len(k_refs)
    masks = _head_lane_masks(GROUP_LANES)
    col = lax.broadcasted_iota(jnp.int32, (Q_BLOCK, K_WINDOW), 1)
    started = col >= BAND_LEFT - j * Q_BLOCK
    for g in range(N_HEADS // HEADS_PER_GROUP):
        lanes = slice(g * GROUP_LANES, (g + 1) * GROUP_LANES)
        qg = q_ref[0, :, lanes]
        acc = jnp.zeros((Q_BLOCK, GROUP_LANES), F32)
        for hh in range(HEADS_PER_GROUP):
            qm = qg * masks[hh].astype(BF16)
            s = jnp.concatenate([_dot_t(qm, k_refs[i][0, :, lanes]) for i in range(nk)], axis=1)
            s = s + bias_ref[g * HEADS_PER_GROUP + hh]
            s = jnp.where(started, s, NEG_INF)
            mx = jnp.max(s, axis=-1, keepdims=True)
            p = jnp.exp(s - mx)
            l = jnp.sum(p, axis=-1, keepdims=True)
            pb = p.astype(BF16)
            pv = _dot(pb[:, 0:Q_BLOCK], v_refs[0][0, :, lanes])
            for i in range(1, nk):
                pv = pv + _dot(pb[:, i * Q_BLOCK:(i + 1) * Q_BLOCK], v_refs[i][0, :, lanes])
            acc = jnp.where(masks[hh], pv * (1.0 / l), acc)
        o_ref[0, :, lanes] = acc.astype(BF16)


def _attn_prompt(q, kpad, vpad, bias):
    B, T, D = q.shape
    nq = T // Q_BLOCK
    blk = lambda off: pl.BlockSpec((1, Q_BLOCK, D), lambda b, j: (b, j + off, 0))
    return pl.pallas_call(
        _attn_prompt_kernel,
        grid=(B, nq),
        in_specs=[blk(0), blk(0), blk(1), blk(2), blk(0), blk(1), blk(2),
                  _resident((N_HEADS, Q_BLOCK, K_WINDOW))],
        out_specs=blk(0),
        out_shape=jax.ShapeDtypeStruct((B, T, D), BF16),
        compiler_params=_params(("arbitrary", "arbitrary")),
        name="attn_prompt",
    )(q, kpad, kpad, kpad, vpad, vpad, vpad, bias)


def _attn_sample_kernel(q_ref, kc_ref, kn_ref, vc_ref, vn_ref, bias_ref, o_ref):
    ts = q_ref.shape[1]
    w = kc_ref.shape[1]
    masks = _head_lane_masks(GROUP_LANES)
    for g in range(N_HEADS // HEADS_PER_GROUP):
        lanes = slice(g * GROUP_LANES, (g + 1) * GROUP_LANES)
        qg = q_ref[0, :, lanes]
        kc = kc_ref[0, :, lanes].astype(BF16)
        kn = kn_ref[0, :, lanes].astype(BF16)
        vc = vc_ref[0, :, lanes].astype(BF16)
        vn = vn_ref[0, :, lanes].astype(BF16)
        acc = jnp.zeros((ts, GROUP_LANES), F32)
        for hh in range(HEADS_PER_GROUP):
            h = g * HEADS_PER_GROUP + hh
            qm = qg * masks[hh].astype(BF16)
            s1 = _dot_t(qm, kc) + bias_ref[h, :, 0:w]
            s2 = _dot_t(qm, kn) + bias_ref[h, :, w:w + ts]
            mx = jnp.maximum(jnp.max(s1, axis=-1, keepdims=True), jnp.max(s2, axis=-1, keepdims=True))
            p1 = jnp.exp(s1 - mx)
            p2 = jnp.exp(s2 - mx)
            l = jnp.sum(p1, axis=-1, keepdims=True) + jnp.sum(p2, axis=-1, keepdims=True)
            pv = _dot(p1.astype(BF16), vc) + _dot(p2.astype(BF16), vn)
            acc = jnp.where(masks[hh], pv * (1.0 / l), acc)
        o_ref[0, :, lanes] = acc.astype(BF16)


def _attn_sample(q, kc, kn, vc, vn, bias):
    nb, ts, D = q.shape
    w = kc.shape[1]
    per = lambda rows: pl.BlockSpec((1, rows, D), lambda b: (b, 0, 0))
    return pl.pallas_call(
        _attn_sample_kernel,
        grid=(nb,),
        in_specs=[per(ts), per(w), per(ts), per(w), per(ts),
                  pl.BlockSpec((N_HEADS, ts, K_WINDOW), lambda b: (0, 0, 0))],
        out_specs=per(ts),
        out_shape=jax.ShapeDtypeStruct((nb, ts, D), BF16),
        compiler_params=_params(("arbitrary",)),
        name="attn_sample",
    )(q, kc, kn, vc, vn, bias)


def kernel(x_prompt, x_sample, cache_conv, cache_k, cache_v, norm_conv, w_pw1, b_pw1, w_dw, b_dw, ln_g, ln_b, w_pw2, b_pw2, norm_kv, w_kv, norm_attn, w_q, w_o, rel_bias, norm_ffn, w_ffn_in, w_ffn_out, norm_final):
    B, T, D = x_prompt.shape
    nb, ts, _ = x_sample.shape
    n_conv = w_pw1.shape[0]
    n_attn = w_q.shape[0]
    w_cache = cache_k.shape[1]
    assert D == D_MODEL and T % CONV_TM == 0 and T % Q_BLOCK == 0 and T >= BAND_LEFT
    assert w_cache == BAND_LEFT and ts <= CHUNK and w_cache + ts <= K_WINDOW

    row = lambda a: a.reshape(1, -1)
    w_pw1b, w_pw2b = w_pw1.astype(BF16), w_pw2.astype(BF16)
    w_kvb, w_qb, w_ob = w_kv.astype(BF16), w_q.astype(BF16), w_o.astype(BF16)
    w_inb, w_outb = w_ffn_in.astype(BF16), w_ffn_out.astype(BF16)

    bias = _bias_bands(rel_bias)

    hist_pad = ((0, 0), (HIST_PAD - CONV_HIST, 0), (0, 0))
    hp = x_prompt
    hs = x_sample.reshape(nb * ts, D)
    conv_p, conv_s = [], []
    for l in range(n_conv):
        cw = (row(norm_conv[l]), w_pw1b[l], row(b_pw1[l]), w_dw[l], row(b_dw[l]),
              row(ln_g[l]), row(ln_b[l]), w_pw2b[l], row(b_pw2[l]))
        hp, st = _conv_prompt(hp, jnp.zeros((B, HIST_PAD, D), F32), *cw)
        conv_p.append(st)
        hs, st = _conv_sample(hs, jnp.pad(cache_conv[l], hist_pad), *cw)
        conv_s.append(st)
        fw = (row(norm_ffn[l]), w_inb[l], w_outb[l])
        hp = _ffn(hp.reshape(B * T, D), *fw).reshape(B, T, D)
        hs = _ffn(hs, *fw)

    kpad, vpad, k_keep, v_keep, qp = _proj_kvq_prompt(
        hp, row(norm_kv), w_kvb, row(norm_attn[0]), w_qb[0])
    k_new, v_new, qs = _proj(hs, row(norm_attn[0]), w_qb[0], row(norm_kv), w_kvb)
    kc = cache_k.reshape(nb, w_cache, D)
    vc = cache_v.reshape(nb, w_cache, D)
    kn = k_new.reshape(nb, ts, D)
    vn = v_new.reshape(nb, ts, D)

    hp = hp.reshape(B * T, D)
    for jl in range(n_attn):
        l = n_conv + jl
        if jl > 0:
            (qp,) = _proj(hp, row(norm_attn[jl]), w_qb[jl])
            (qs,) = _proj(hs, row(norm_attn[jl]), w_qb[jl])
        op = _attn_prompt(qp.reshape(B, T, D), kpad, vpad, bias[jl])
        os_ = _attn_sample(qs.reshape(nb, ts, D), kc, kn, vc, vn, bias[jl])
        last = jl == n_attn - 1
        fw = dict(g=row(norm_ffn[l]), w_in=w_inb[l], w_out=w_outb[l], w_o=w_ob[jl],
                  g_final=row(norm_final) if last else None)
        hp = _ffn(hp, o=op.reshape(B * T, D), **fw)
        hs = _ffn(hs, o=os_.reshape(nb * ts, D), **fw)

    y_prompt = hp.reshape(B, T, D)
    y_sample = hs.reshape(nb, ts, D)
    heads = lambda a, n, r: a.reshape(n, r, N_HEADS, HEAD_DIM)
    return (y_prompt, y_sample,
            jnp.stack(conv_p, axis=0), jnp.stack(conv_s, axis=0),
            heads(k_keep, B, BAND_LEFT), heads(v_keep, B, BAND_LEFT),
            heads(k_new, nb, ts), heads(v_new, nb, ts))
```

```python
import functools

import jax
import jax.numpy as jnp
from jax import lax
from jax.experimental import pallas as pl
from jax.experimental.pallas import tpu as pltpu

D_MODEL = 1024
N_HEADS = 16
HEAD_DIM = 64
FFN_HIDDEN = 2816
CHUNK = 64
BAND_LEFT = 512
CONV_WIDTH = 31
CONV_HIST = 30
HIST_PAD = 32
REL_CLIP = 128
N_REL = 2 * REL_CLIP + 1
EPS = 1e-6
NEG_INF = -1e30

Q_BLOCK = 256
K_WINDOW = Q_BLOCK + BAND_LEFT
HEADS_PER_GROUP = 4
GROUP_LANES = HEADS_PER_GROUP * HEAD_DIM
ROW_EXT = 1024

VMEM_LIMIT = 56 * 1024 * 1024

F32 = jnp.float32
BF16 = jnp.bfloat16


def _resident(shape):
    nd = len(shape)
    return pl.BlockSpec(shape, lambda *_: (0,) * nd, pipeline_mode=pl.Buffered(1))


def _params(sem):
    return pltpu.CompilerParams(dimension_semantics=sem, vmem_limit_bytes=VMEM_LIMIT)


def _rms(x, g):
    ms = jnp.mean(x * x, axis=-1, keepdims=True)
    return x * lax.rsqrt(ms + EPS) * g


def _sigmoid(x):
    return 1.0 / (1.0 + jnp.exp(-x))


def _dot(a, b):
    return jnp.dot(a, b, preferred_element_type=F32)


def _dot_t(a, b):
    return lax.dot_general(a, b, (((1,), (1,)), ((), ())), preferred_element_type=F32)


def _layer_norm_swish(acc, ln_g, ln_b):
    mu = jnp.mean(acc, axis=-1, keepdims=True)
    xc = acc - mu
    var = jnp.mean(xc * xc, axis=-1, keepdims=True)
    y = xc * lax.rsqrt(var + EPS) * ln_g + ln_b
    return y * _sigmoid(y)


CONV_TM = 512
CONV_ROWS_MM = 256
LANES = 128
N_SLABS = D_MODEL // LANES
SUBLANES = 8
CONV_SEG = CONV_TM // SUBLANES
CONV_HALO = SUBLANES * CONV_HIST
CONV_PITCH = CONV_SEG + SUBLANES
CONV_ROWS_DW = 64


def _conv_prompt_kernel(h_ref, hist_ref, g_ref, w1_ref, b1_ref, wdw_ref, bdw_ref,
                        lng_ref, lnb_ref, w2_ref, b2_ref,
                        out_ref, state_ref, fbuf, cbuf, ybuf, gtail):
    t = pl.program_id(1)
    tm = CONV_TM

    @pl.when((pl.program_id(0) == 0) & (t == 0))
    def _():
        ybuf[...] = jnp.zeros_like(ybuf)

    @pl.when(t == 0)
    def _():
        gtail[...] = hist_ref[0]

    g = g_ref[...]
    b1 = b1_ref[...]
    segs_per_mm = CONV_ROWS_MM // CONV_SEG
    tail = None
    for c in range(tm // CONV_ROWS_MM):
        rows = slice(c * CONV_ROWS_MM, (c + 1) * CONV_ROWS_MM)
        xn = _rms(h_ref[0, rows, :], g).astype(BF16)
        ag = _dot(xn, w1_ref[...]) + b1
        glu = ag[:, :D_MODEL] * _sigmoid(ag[:, D_MODEL:])
        for s in range(segs_per_mm):
            j = c * segs_per_mm + s
            for sl in range(N_SLABS):
                fbuf[sl, pl.ds(CONV_HALO + j, CONV_SEG, stride=SUBLANES), :] = (
                    glu[s * CONV_SEG:(s + 1) * CONV_SEG, sl * LANES:(sl + 1) * LANES])
        tail = glu[CONV_ROWS_MM - HIST_PAD:, :]

    prev = gtail[HIST_PAD - CONV_HIST:, :]
    for sl in range(N_SLABS):
        src = CONV_HALO + SUBLANES * (CONV_SEG - CONV_HIST) - 1
        fbuf[sl, 0:CONV_HALO, :] = fbuf[sl, src:src + CONV_HALO, :]
        fbuf[sl, pl.ds(0, CONV_HIST, stride=SUBLANES), :] = prev[:, sl * LANES:(sl + 1) * LANES]
    gtail[...] = tail

    @pl.when(t == pl.num_programs(1) - 1)
    def _():
        state_ref[0] = tail[HIST_PAD - CONV_HIST:, :]

    n_dw = tm // CONV_ROWS_DW
    for sl in range(N_SLABS):
        lanes = slice(sl * LANES, (sl + 1) * LANES)
        bdw = bdw_ref[:, lanes]

        def dw_body(i, carry, sl=sl, lanes=lanes, bdw=bdw):
            r0 = pl.multiple_of(i * CONV_ROWS_DW, CONV_ROWS_DW)
            acc = jnp.zeros((CONV_ROWS_DW, LANES), F32) + bdw
            for k in range(CONV_WIDTH):
                acc = acc + fbuf[sl, pl.ds(r0 + SUBLANES * k, CONV_ROWS_DW), :] * wdw_ref[k:k + 1, lanes]
            cbuf[sl, pl.ds(r0, CONV_ROWS_DW), :] = acc
            return carry

        lax.fori_loop(0, n_dw, dw_body, 0)

    ln_g = lng_ref[...]
    ln_b = lnb_ref[...]
    per_step = CONV_ROWS_DW // SUBLANES

    def ln_body(i, carry):
        r0 = pl.multiple_of(i * CONV_ROWS_DW, CONV_ROWS_DW)
        acc = jnp.concatenate([cbuf[sl, pl.ds(r0, CONV_ROWS_DW), :] for sl in range(N_SLABS)], axis=1)
        y = _layer_norm_swish(acc, ln_g, ln_b)
        for q in range(per_step):
            for sl in range(N_SLABS):
                ybuf[sl, pl.ds(i * per_step + q, SUBLANES, stride=CONV_PITCH), :] = (
                    y[q * SUBLANES:(q + 1) * SUBLANES, sl * LANES:(sl + 1) * LANES])
        return carry

    lax.fori_loop(0, n_dw, ln_body, 0)

    yb = jnp.concatenate([ybuf[sl] for sl in range(N_SLABS)], axis=1).astype(BF16)
    z = _dot(yb, w2_ref[...])
    b2 = b2_ref[...]
    for j in range(SUBLANES):
        rows = slice(j * CONV_SEG, (j + 1) * CONV_SEG)
        out_ref[0, rows, :] = h_ref[0, rows, :] + z[j * CONV_PITCH:j * CONV_PITCH + CONV_SEG, :] + b2


def _conv_prompt(h, hist, g, w1, b1, wdw, bdw, ln_g, ln_b, w2, b2):
    B, T, D = h.shape
    tm = CONV_TM
    vec = lambda n: _resident((1, n))
    return pl.pallas_call(
        _conv_prompt_kernel,
        grid=(B, T // tm),
        in_specs=[
            pl.BlockSpec((1, tm, D), lambda b, t: (b, t, 0)),
            pl.BlockSpec((1, HIST_PAD, D), lambda b, t: (b, 0, 0)),
            vec(D), _resident((D, 2 * D)), vec(2 * D), _resident((CONV_WIDTH, D)), vec(D),
            vec(D), vec(D), _resident((D, D)), vec(D),
        ],
        out_specs=[
            pl.BlockSpec((1, tm, D), lambda b, t: (b, t, 0)),
            pl.BlockSpec((1, CONV_HIST, D), lambda b, t: (b, 0, 0)),
        ],
        out_shape=[
            jax.ShapeDtypeStruct((B, T, D), F32),
            jax.ShapeDtypeStruct((B, CONV_HIST, D), F32),
        ],
        scratch_shapes=[
            pltpu.VMEM((N_SLABS, CONV_HALO + tm, LANES), F32),
            pltpu.VMEM((N_SLABS, tm, LANES), F32),
            pltpu.VMEM((N_SLABS, SUBLANES * CONV_PITCH, LANES), F32),
            pltpu.VMEM((HIST_PAD, D), F32),
        ],
        compiler_params=_params(("arbitrary", "arbitrary")),
        name="conv_prompt",
    )(h, hist, g, w1, b1, wdw, bdw, ln_g, ln_b, w2, b2)


def _conv_sample_kernel(h_ref, hist_ref, g_ref, w1_ref, b1_ref, wdw_ref, bdw_ref,
                        lng_ref, lnb_ref, w2_ref, b2_ref,
                        out_ref, state_ref, fbuf):
    nb, _, _ = hist_ref.shape
    ts = h_ref.shape[0] // nb
    x = h_ref[...]
    xn = _rms(x, g_ref[...]).astype(BF16)
    ag = _dot(xn, w1_ref[...]) + b1_ref[...]
    glu = ag[:, :D_MODEL] * _sigmoid(ag[:, D_MODEL:])
    fbuf[:, 0:HIST_PAD, :] = hist_ref[...]
    fbuf[:, HIST_PAD:HIST_PAD + ts, :] = glu.reshape(nb, ts, D_MODEL)
    acc = jnp.zeros((nb, ts, D_MODEL), F32) + bdw_ref[...]
    for k in range(CONV_WIDTH):
        lo = HIST_PAD - CONV_HIST + k
        acc = acc + fbuf[:, lo:lo + ts, :] * wdw_ref[k:k + 1, :]
    y = _layer_norm_swish(acc.reshape(nb * ts, D_MODEL), lng_ref[...], lnb_ref[...]).astype(BF16)
    out_ref[...] = x + _dot(y, w2_ref[...]) + b2_ref[...]
    state_ref[...] = fbuf[:, HIST_PAD + ts - CONV_HIST:HIST_PAD + ts, :]


def _conv_sample(h, hist, g, w1, b1, wdw, bdw, ln_g, ln_b, w2, b2):
    M, D = h.shape
    nb = hist.shape[0]
    ts = M // nb
    vec = lambda n: _resident((1, n))
    return pl.pallas_call(
        _conv_sample_kernel,
        grid=(1,),
        in_specs=[
            _resident((M, D)), _resident((nb, HIST_PAD, D)),
            vec(D), _resident((D, 2 * D)), vec(2 * D), _resident((CONV_WIDTH, D)), vec(D),
            vec(D), vec(D), _resident((D, D)), vec(D),
        ],
        out_specs=[
            pl.BlockSpec((M, D), lambda i: (0, 0)),
            pl.BlockSpec((nb, CONV_HIST, D), lambda i: (0, 0, 0)),
        ],
        out_shape=[
            jax.ShapeDtypeStruct((M, D), F32),
            jax.ShapeDtypeStruct((nb, CONV_HIST, D), F32),
        ],
        scratch_shapes=[pltpu.VMEM((nb, HIST_PAD + ts, D), F32)],
        compiler_params=_params(("arbitrary",)),
        name="conv_sample",
    )(h, hist, g, w1, b1, wdw, bdw, ln_g, ln_b, w2, b2)


FFN_TM = 512
FFN_FC = 256


def _ffn_kernel(*refs, with_o, with_final):
    it = iter(refs)
    h_ref = next(it)
    o_ref = next(it) if with_o else None
    wo_ref = next(it) if with_o else None
    g_ref = next(it)
    win_ref = next(it)
    wout_ref = next(it)
    gf_ref = next(it) if with_final else None
    out_ref = next(it)
    gbuf = next(it)

    x = h_ref[...]
    if with_o:
        x = x + _dot(o_ref[...], wo_ref[...])
    xn = _rms(x, g_ref[...]).astype(BF16)
    for c in range(FFN_HIDDEN // FFN_FC):
        a = _dot(xn, win_ref[:, c * FFN_FC:(c + 1) * FFN_FC])
        b = _dot(xn, win_ref[:, FFN_HIDDEN + c * FFN_FC:FFN_HIDDEN + (c + 1) * FFN_FC])
        gbuf[:, c * FFN_FC:(c + 1) * FFN_FC] = (a * _sigmoid(a) * b).astype(BF16)
    y = x + _dot(gbuf[...], wout_ref[...])
    if with_final:
        y = _rms(y, gf_ref[...])
    out_ref[...] = y


def _ffn(h, g, w_in, w_out, o=None, w_o=None, g_final=None):
    M, D = h.shape
    tm = min(FFN_TM, M)
    with_o = o is not None
    with_final = g_final is not None
    row = pl.BlockSpec((tm, D), lambda i: (i, 0))
    args = [h]
    specs = [row]
    if with_o:
        args += [o, w_o]
        specs += [row, _resident((D, D))]
    args += [g, w_in, w_out]
    specs += [_resident((1, D)), _resident((D, 2 * FFN_HIDDEN)), _resident((FFN_HIDDEN, D))]
    if with_final:
        args.append(g_final)
        specs.append(_resident((1, D)))
    return pl.pallas_call(
        functools.partial(_ffn_kernel, with_o=with_o, with_final=with_final),
        grid=(M // tm,),
        in_specs=specs,
        out_specs=row,
        out_shape=jax.ShapeDtypeStruct((M, D), F32),
        scratch_shapes=[pltpu.VMEM((tm, FFN_HIDDEN), BF16)],
        compiler_params=_params(("arbitrary",)),
        name="ffn",
    )(*args)


PROJ_TM = 512


def _proj_kvq_prompt_kernel(h_ref, gkv_ref, wkv_ref, gq_ref, wq_ref,
                            kpad_ref, vpad_ref, kf_ref, vf_ref, q_ref):
    t = pl.program_id(1)

    @pl.when(t == 0)
    def _():
        kpad_ref[...] = jnp.zeros_like(kpad_ref)
        vpad_ref[...] = jnp.zeros_like(vpad_ref)

    @pl.when(t > 0)
    def _():
        x = h_ref[0]
        ms = jnp.mean(x * x, axis=-1, keepdims=True)
        xr = x * lax.rsqrt(ms + EPS)
        kv = _dot((xr * gkv_ref[...]).astype(BF16), wkv_ref[...])
        k = kv[:, :D_MODEL]
        v = kv[:, D_MODEL:]
        kpad_ref[0] = k.astype(BF16)
        vpad_ref[0] = v.astype(BF16)
        q = _dot((xr * gq_ref[...]).astype(BF16), wq_ref[...])
        q_ref[0] = (q * (HEAD_DIM ** -0.5)).astype(BF16)

        @pl.when(t == pl.num_programs(1) - 1)
        def _():
            kf_ref[0] = k
            vf_ref[0] = v


def _proj_kvq_prompt(h, g_kv, w_kv, g_q, w_q):
    B, T, D = h.shape
    tm = PROJ_TM
    assert tm == BAND_LEFT
    nt = T // tm
    cur = lambda b, t: (b, jnp.maximum(t - 1, 0), 0)
    return pl.pallas_call(
        _proj_kvq_prompt_kernel,
        grid=(B, nt + 1),
        in_specs=[
            pl.BlockSpec((1, tm, D), cur),
            _resident((1, D)), _resident((D, 2 * D)), _resident((1, D)), _resident((D, D)),
        ],
        out_specs=[
            pl.BlockSpec((1, tm, D), lambda b, t: (b, t, 0)),
            pl.BlockSpec((1, tm, D), lambda b, t: (b, t, 0)),
            pl.BlockSpec((1, tm, D), lambda b, t: (b, 0, 0)),
            pl.BlockSpec((1, tm, D), lambda b, t: (b, 0, 0)),
            pl.BlockSpec((1, tm, D), cur),
        ],
        out_shape=[
            jax.ShapeDtypeStruct((B, T + BAND_LEFT, D), BF16),
            jax.ShapeDtypeStruct((B, T + BAND_LEFT, D), BF16),
            jax.ShapeDtypeStruct((B, tm, D), F32),
            jax.ShapeDtypeStruct((B, tm, D), F32),
            jax.ShapeDtypeStruct((B, T, D), BF16),
        ],
        compiler_params=_params(("arbitrary", "arbitrary")),
        name="proj_kvq_prompt",
    )(h, g_kv, w_kv, g_q, w_q)


def _proj_kernel(*refs, with_kv):
    it = iter(refs)
    h_ref = next(it)
    if with_kv:
        gkv_ref = next(it)
        wkv_ref = next(it)
    gq_ref = next(it)
    wq_ref = next(it)
    if with_kv:
        k_ref = next(it)
        v_ref = next(it)
    q_ref = next(it)
    x = h_ref[...]
    ms = jnp.mean(x * x, axis=-1, keepdims=True)
    xr = x * lax.rsqrt(ms + EPS)
    if with_kv:
        kv = _dot((xr * gkv_ref[...]).astype(BF16), wkv_ref[...])
        k_ref[...] = kv[:, :D_MODEL]
        v_ref[...] = kv[:, D_MODEL:]
    q = _dot((xr * gq_ref[...]).astype(BF16), wq_ref[...])
    q_ref[...] = (q * (HEAD_DIM ** -0.5)).astype(BF16)


def _proj(h, g_q, w_q, g_kv=None, w_kv=None):
    M, D = h.shape
    tm = min(PROJ_TM, M)
    with_kv = g_kv is not None
    row = pl.BlockSpec((tm, D), lambda i: (i, 0))
    args = [h]
    specs = [row]
    if with_kv:
        args += [g_kv, w_kv]
        specs += [_resident((1, D)), _resident((D, 2 * D))]
    args += [g_q, w_q]
    specs += [_resident((1, D)), _resident((D, D))]
    out_specs = [row]
    out_shape = [jax.ShapeDtypeStruct((M, D), BF16)]
    if with_kv:
        out_specs = [row, row] + out_specs
        out_shape = [jax.ShapeDtypeStruct((M, D), F32)] * 2 + out_shape
    return pl.pallas_call(
        functools.partial(_proj_kernel, with_kv=with_kv),
        grid=(M // tm,),
        in_specs=specs,
        out_specs=out_specs,
        out_shape=out_shape,
        compiler_params=_params(("arbitrary",)),
        name="proj",
    )(*args)


def _bias_kernel(tbl_ref, out_ref, rows_scr):
    m = lax.broadcasted_iota(jnp.int32, (N_HEADS, ROW_EXT), 1)
    mm = jnp.where(m < K_WINDOW, m, m - ROW_EXT)
    idx = jnp.clip(BAND_LEFT - mm, -REL_CLIP, REL_CLIP) + REL_CLIP
    rows = jnp.zeros((N_HEADS, ROW_EXT), F32)
    for t in range(N_REL):
        rows = jnp.where(idx == t, tbl_ref[0, :, t:t + 1], rows)
    rows_scr[...] = rows

    qi = lax.broadcasted_iota(jnp.int32, (Q_BLOCK, K_WINDOW), 0) // CHUNK
    kj = lax.broadcasted_iota(jnp.int32, (Q_BLOCK, K_WINDOW), 1) // CHUNK
    in_band = (kj >= qi) & (kj <= qi + BAND_LEFT // CHUNK)
    for h in range(N_HEADS):
        row = jnp.broadcast_to(rows_scr[h:h + 1, :], (Q_BLOCK, ROW_EXT))
        toe = pltpu.roll(row, 0, 1, stride=1, stride_axis=0)
        out_ref[0, h] = jnp.where(in_band, toe[:, :K_WINDOW], NEG_INF)


def _bias_bands(rel_bias):
    L = rel_bias.shape[0]
    return pl.pallas_call(
        _bias_kernel,
        grid=(L,),
        in_specs=[pl.BlockSpec((1, N_HEADS, N_REL), lambda l: (l, 0, 0))],
        out_specs=pl.BlockSpec((1, N_HEADS, Q_BLOCK, K_WINDOW), lambda l: (l, 0, 0, 0)),
        out_shape=jax.ShapeDtypeStruct((L, N_HEADS, Q_BLOCK, K_WINDOW), F32),
        scratch_shapes=[pltpu.VMEM((N_HEADS, ROW_EXT), F32)],
        compiler_params=_params(("arbitrary",)),
        name="bias_bands",
    )(rel_bias)


def _head_lane_masks(width):
    lane = lax.broadcasted_iota(jnp.int32, (1, width), 1)
    return [(lane >= hh * HEAD_DIM) & (lane < (hh + 1) * HEAD_DIM) for hh in range(HEADS_PER_GROUP)]


def _attn_prompt_kernel(q_ref, k0_ref, k1_ref, k2_ref, v0_ref, v1_ref, v2_ref, bias_ref, o_ref):
    j = pl.program_id(1)
    n_clipped = BAND_LEFT // Q_BLOCK

    @pl.when(j < n_clipped)
    def _():
        _attn_prompt_body(q_ref, k0_ref, k1_ref, k2_ref, v0_ref, v1_ref, v2_ref, bias_ref, o_ref, j)

    @pl.when(j >= n_clipped)
    def _():
        _attn_prompt_body(q_ref, k0_ref, k1_ref, k2_ref, v0_ref, v1_ref, v2_ref, bias_ref, o_ref, None)


def _attn_prompt_body(q_ref, k0_ref, k1_ref, k2_ref, v0_ref, v1_ref, v2_ref, bias_ref, o_ref, clip_j):
    k_refs = (k0_ref, k1_ref, k2_ref)
    v_refs = (v0_ref, v1_ref, v2_ref)
    nk = len(k_refs)
    masks = _head_lane_masks(GROUP_LANES)
    if clip_j is not None:
        col = lax.broadcasted_iota(jnp.int32, (Q_BLOCK, K_WINDOW), 1)
        started = col >= BAND_LEFT - clip_j * Q_BLOCK
    for g in range(N_HEADS // HEADS_PER_GROUP):
        lanes = slice(g * GROUP_LANES, (g + 1) * GROUP_LANES)
        qg = q_ref[0, :, lanes]
        acc = jnp.zeros((Q_BLOCK, GROUP_LANES), F32)
        for hh in range(HEADS_PER_GROUP):
            qm = qg * masks[hh].astype(BF16)
            s = jnp.concatenate([_dot_t(qm, k_refs[i][0, :, lanes]) for i in range(nk)], axis=1)
            s = s + bias_ref[g * HEADS_PER_GROUP + hh]
            if clip_j is not None:
                s = jnp.where(started, s, NEG_INF)
            mx = jnp.max(s, axis=-1, keepdims=True)
            p = jnp.exp(s - mx)
            l = jnp.sum(p, axis=-1, keepdims=True)
            pb = p.astype(BF16)
            pv = _dot(pb[:, 0:Q_BLOCK], v_refs[0][0, :, lanes])
            for i in range(1, nk):
                pv = pv + _dot(pb[:, i * Q_BLOCK:(i + 1) * Q_BLOCK], v_refs[i][0, :, lanes])
            acc = jnp.where(masks[hh], pv * (1.0 / l), acc)
        o_ref[0, :, lanes] = acc.astype(BF16)


def _attn_prompt(q, kpad, vpad, bias):
    B, T, D = q.shape
    nq = T // Q_BLOCK
    blk = lambda off: pl.BlockSpec((1, Q_BLOCK, D), lambda b, j: (b, j + off, 0))
    return pl.pallas_call(
        _attn_prompt_kernel,
        grid=(B, nq),
        in_specs=[blk(0), blk(0), blk(1), blk(2), blk(0), blk(1), blk(2),
                  _resident((N_HEADS, Q_BLOCK, K_WINDOW))],
        out_specs=blk(0),
        out_shape=jax.ShapeDtypeStruct((B, T, D), BF16),
        compiler_params=_params(("arbitrary", "arbitrary")),
        name="attn_prompt",
    )(q, kpad, kpad, kpad, vpad, vpad, vpad, bias)


def _attn_sample_kernel(q_ref, kc_ref, kn_ref, vc_ref, vn_ref, bias_ref, o_ref):
    ts = q_ref.shape[1]
    w = kc_ref.shape[1]
    masks = _head_lane_masks(GROUP_LANES)
    for g in range(N_HEADS // HEADS_PER_GROUP):
        lanes = slice(g * GROUP_LANES, (g + 1) * GROUP_LANES)
        qg = q_ref[0, :, lanes]
        kc = kc_ref[0, :, lanes].astype(BF16)
        kn = kn_ref[0, :, lanes].astype(BF16)
        vc = vc_ref[0, :, lanes].astype(BF16)
        vn = vn_ref[0, :, lanes].astype(BF16)
        heads = range(g * HEADS_PER_GROUP, (g + 1) * HEADS_PER_GROUP)
        qm = jnp.concatenate([qg * masks[hh].astype(BF16) for hh in range(HEADS_PER_GROUP)], axis=0)
        b1 = jnp.concatenate([bias_ref[h, :, 0:w] for h in heads], axis=0)
        b2 = jnp.concatenate([bias_ref[h, :, w:w + ts] for h in heads], axis=0)
        s1 = _dot_t(qm, kc) + b1
        s2 = _dot_t(qm, kn) + b2
        mx = jnp.maximum(jnp.max(s1, axis=-1, keepdims=True), jnp.max(s2, axis=-1, keepdims=True))
        p1 = jnp.exp(s1 - mx)
        p2 = jnp.exp(s2 - mx)
        l = jnp.sum(p1, axis=-1, keepdims=True) + jnp.sum(p2, axis=-1, keepdims=True)
        pv = (_dot(p1.astype(BF16), vc) + _dot(p2.astype(BF16), vn)) * (1.0 / l)
        acc = jnp.zeros((ts, GROUP_LANES), F32)
        for hh in range(HEADS_PER_GROUP):
            acc = jnp.where(masks[hh], pv[hh * ts:(hh + 1) * ts, :], acc)
        o_ref[0, :, lanes] = acc.astype(BF16)


def _attn_sample(q, kc, kn, vc, vn, bias):
    nb, ts, D = q.shape
    w = kc.shape[1]
    per = lambda rows: pl.BlockSpec((1, rows, D), lambda b: (b, 0, 0))
    return pl.pallas_call(
        _attn_sample_kernel,
        grid=(nb,),
        in_specs=[per(ts), per(w), per(ts), per(w), per(ts),
                  pl.BlockSpec((N_HEADS, ts, K_WINDOW), lambda b: (0, 0, 0))],
        out_specs=per(ts),
        out_shape=jax.ShapeDtypeStruct((nb, ts, D), BF16),
        compiler_params=_params(("arbitrary",)),
        name="attn_sample",
    )(q, kc, kn, vc, vn, bias)


def kernel(x_prompt, x_sample, cache_conv, cache_k, cache_v, norm_conv, w_pw1, b_pw1, w_dw, b_dw, ln_g, ln_b, w_pw2, b_pw2, norm_kv, w_kv, norm_attn, w_q, w_o, rel_bias, norm_ffn, w_ffn_in, w_ffn_out, norm_final):
    B, T, D = x_prompt.shape
    nb, ts, _ = x_sample.shape
    n_conv = w_pw1.shape[0]
    n_attn = w_q.shape[0]
    w_cache = cache_k.shape[1]
    assert D == D_MODEL and T % CONV_TM == 0 and T % Q_BLOCK == 0 and T >= BAND_LEFT
    assert w_cache == BAND_LEFT and ts <= CHUNK and w_cache + ts <= K_WINDOW

    row = lambda a: a.reshape(1, -1)
    w_pw1b, w_pw2b = w_pw1.astype(BF16), w_pw2.astype(BF16)
    w_kvb, w_qb, w_ob = w_kv.astype(BF16), w_q.astype(BF16), w_o.astype(BF16)
    w_inb, w_outb = w_ffn_in.astype(BF16), w_ffn_out.astype(BF16)

    bias = _bias_bands(rel_bias)

    hist_pad = ((0, 0), (HIST_PAD - CONV_HIST, 0), (0, 0))
    hp = x_prompt
    hs = x_sample.reshape(nb * ts, D)
    conv_p, conv_s = [], []
    for l in range(n_conv):
        cw = (row(norm_conv[l]), w_pw1b[l], row(b_pw1[l]), w_dw[l], row(b_dw[l]),
              row(ln_g[l]), row(ln_b[l]), w_pw2b[l], row(b_pw2[l]))
        hp, st = _conv_prompt(hp, jnp.zeros((B, HIST_PAD, D), F32), *cw)
        conv_p.append(st)
        hs, st = _conv_sample(hs, jnp.pad(cache_conv[l], hist_pad), *cw)
        conv_s.append(st)
        fw = (row(norm_ffn[l]), w_inb[l], w_outb[l])
        hp = _ffn(hp.reshape(B * T, D), *fw).reshape(B, T, D)
        hs = _ffn(hs, *fw)

    kpad, vpad, k_keep, v_keep, qp = _proj_kvq_prompt(
        hp, row(norm_kv), w_kvb, row(norm_attn[0]), w_qb[0])
    k_new, v_new, qs = _proj(hs, row(norm_attn[0]), w_qb[0], row(norm_kv), w_kvb)
    kc = cache_k.reshape(nb, w_cache, D)
    vc = cache_v.reshape(nb, w_cache, D)
    kn = k_new.reshape(nb, ts, D)
    vn = v_new.reshape(nb, ts, D)

    hp = hp.reshape(B * T, D)
    for jl in range(n_attn):
        l = n_conv + jl
        if jl > 0:
            (qp,) = _proj(hp, row(norm_attn[jl]), w_qb[jl])
            (qs,) = _proj(hs, row(norm_attn[jl]), w_qb[jl])
        op = _attn_prompt(qp.reshape(B, T, D), kpad, vpad, bias[jl])
        os_ = _attn_sample(qs.reshape(nb, ts, D), kc, kn, vc, vn, bias[jl])
        last = jl == n_attn - 1
        fw = dict(g=row(norm_ffn[l]), w_in=w_inb[l], w_out=w_outb[l], w_o=w_ob[jl],
                  g_final=row(norm_final) if last else None)
        hp = _ffn(hp, o=op.reshape(B * T, D), **fw)
        hs = _ffn(hs, o=os_.reshape(nb * ts, D), **fw)

    y_prompt = hp.reshape(B, T, D)
    y_sample = hs.reshape(nb, ts, D)
    heads = lambda a, n, r: a.reshape(n, r, N_HEADS, HEAD_DIM)
    return (y_prompt, y_sample,
            jnp.stack(conv_p, axis=0), jnp.stack(conv_s, axis=0),
            heads(k_keep, B, BAND_LEFT), heads(v_keep, B, BAND_LEFT),
            heads(k_new, nb, ts), heads(v_new, nb, ts))
```
